```python
import math
import jax, jax.numpy as jnp
from jax import lax
import numpy as np

D_MODEL = 1024
BATCH = 2
SEQ = 8192
DEPTH = 4
DEC_BATCH = 32
DEC_SEQ = 4
PAST_LEN = 8192
PAGE_SIZE = 128

N_A = DEPTH // 2
N_B = DEPTH - N_A
H_A = 8
DH_A = D_MODEL // H_A
D_A = H_A * DH_A
MLSTM_CHUNK = 64
F_BIAS_LO = 3.0
F_BIAS_HI = 6.0
H_B = 8
DK_B = D_MODEL // (2 * H_B)
DV_B = 2 * DK_B
D_B = H_B * DV_B
QK_B = H_B * 2 * DK_B
Q_BLOCK = 128
ATTN_SCALE = DK_B ** -0.5
ROPE_THETA = 10000.0
EPS = 1e-6
A_IN = 5 * D_A + 2 * H_A
B_IN = QK_B + D_B
KV_OUT = QK_B + D_B

kernel_name = 'yoco_mlstm_diffattn_step'

F32 = jnp.float32


def rmsnorm(x, g):
    xf = x.astype(F32)
    y = xf * lax.rsqrt(jnp.mean(xf * xf, axis=-1, keepdims=True) + EPS)
    return (y * g.astype(F32)).astype(x.dtype)


def head_rmsnorm(h, g):
    hf = h.astype(F32)
    return hf * lax.rsqrt(jnp.mean(hf * hf, axis=-1, keepdims=True) + EPS) * g.astype(F32)


def rope(x, pos):
    half = DK_B // 2
    inv = ROPE_THETA ** (-(jnp.arange(half, dtype=F32) / half))
    ang = pos[:, None] * inv[None, :]
    cos = jnp.cos(ang)[None, :, None, None, :]
    sin = jnp.sin(ang)[None, :, None, None, :]
    xf = x.astype(F32)
    x1, x2 = xf[..., :half], xf[..., half:]
    return jnp.concatenate([x1 * cos - x2 * sin, x2 * cos + x1 * sin], axis=-1).astype(x.dtype)


def mlstm_cell(q, k, v, i_pre, log_f, c0, n0, m0):
    bsz, t_, h_, d_ = q.shape
    L = MLSTM_CHUNK if t_ % MLSTM_CHUNK == 0 else t_
    nc = t_ // L

    def to_chunks(a):
        a = a.reshape((bsz, nc, L, h_) + a.shape[3:])
        return jnp.moveaxis(a, (1, 3), (0, 2))

    tril = jnp.tril(jnp.ones((L, L), dtype=bool))

    def body(carry, xs):
        c, n, m = carry
        qc, kc, vc, ic, fc = xs
        bcum = jnp.cumsum(fc, axis=-1)
        d = bcum[..., :, None] - bcum[..., None, :] + ic[..., None, :]
        d = jnp.where(tril, d, -jnp.inf)
        inter = bcum + m[..., None]
        mj = jnp.maximum(inter, jnp.max(d, axis=-1))
        w_inter = jnp.exp(inter - mj)
        a = jnp.exp(d - mj[..., None]) * jnp.einsum('bhjd,bhsd->bhjs', qc, kc)
        num = w_inter[..., None] * jnp.einsum('bhvk,bhjk->bhjv', c, qc) + jnp.einsum('bhjs,bhsv->bhjv', a, vc)
        den = w_inter * jnp.einsum('bhk,bhjk->bhj', n, qc) + jnp.sum(a, axis=-1)
        h = num / jnp.maximum(jnp.abs(den), jnp.exp(-mj))[..., None]
        b_last = bcum[..., -1]
        g = b_last[..., None] - bcum + ic
        m_new = jnp.maximum(b_last + m, jnp.max(g, axis=-1))
        decay = jnp.exp(b_last + m - m_new)
        wg = jnp.exp(g - m_new[..., None])
        c_new = decay[..., None, None] * c + jnp.einsum('bhs,bhsv,bhsk->bhvk', wg, vc, kc)
        n_new = decay[..., None] * n + jnp.einsum('bhs,bhsk->bhk', wg, kc)
        return (c_new, n_new, m_new), h

    xs = (to_chunks(q), to_chunks(k), to_chunks(v), to_chunks(i_pre), to_chunks(log_f))
    (c1, n1, m1), hs = lax.scan(body, (c0, n0, m0), xs)
    hs = jnp.moveaxis(hs, (0, 2), (1, 3)).reshape(bsz, t_, h_, d_)
    return hs, c1, n1, m1


def mlstm_layer(x, g, w_in, b_i, b_f, hnorm, w_out, c0, n0, m0):
    bsz, t_, _ = x.shape
    p = rmsnorm(x, g) @ w_in
    q, k, v, o, z, gi, gf = jnp.split(p, [D_A, 2 * D_A, 3 * D_A, 4 * D_A, 5 * D_A, 5 * D_A + H_A], axis=-1)
    heads = lambda a: a.reshape(bsz, t_, H_A, DH_A).astype(F32)
    i_pre = gi.astype(F32) + b_i.astype(F32)
    log_f = jax.nn.log_sigmoid(gf.astype(F32) + b_f.astype(F32))
    hc, c1, n1, m1 = mlstm_cell(heads(q), heads(k) * (DH_A ** -0.5), heads(v), i_pre, log_f,
                                c0.astype(F32), n0.astype(F32), m0.astype(F32))
    hc = jax.nn.sigmoid(heads(o)) * hc
    hc = head_rmsnorm(hc, hnorm.reshape(H_A, DH_A))
    y = (hc.reshape(bsz, t_, D_A).astype(x.dtype) * jax.nn.silu(z)) @ w_out
    return x + y, c1, n1, m1


def shared_kv(x, g, w_kv, pos):
    bsz, t_, _ = x.shape
    p = rmsnorm(x, g) @ w_kv
    k = rope(p[..., :QK_B].reshape(bsz, t_, H_B, 2, DK_B), pos)
    v = p[..., QK_B:].reshape(bsz, t_, H_B, DV_B)
    return k, v


def diff_mix(q, k_parts, v_parts, masks, lam):
    scores = []
    for kp, mk in zip(k_parts, masks):
        s = jnp.einsum('bqhcd,bkhcd->bhcqk', q, kp).astype(F32) * ATTN_SCALE
        if mk is not None:
            s = jnp.where(mk, s, -jnp.inf)
        scores.append(s)
    p = jax.nn.softmax(jnp.concatenate(scores, axis=-1), axis=-1)
    w = p[:, :, 0] - lam * p[:, :, 1]
    out = None
    start = 0
    for vp in v_parts:
        s_len = vp.shape[1]
        o = jnp.einsum('bhqk,bkhe->bqhe', w[..., start:start + s_len].astype(vp.dtype), vp)
        out = o if out is None else out + o
        start += s_len
    return out


def make_attend(k_new, v_new, past):
    if past is None:
        def attend(q, lam):
            bsz, t_ = q.shape[:2]
            nb = t_ // Q_BLOCK
            qb = jnp.moveaxis(q.reshape(bsz, nb, Q_BLOCK, H_B, 2, DK_B), 1, 0)
            kpos = jnp.arange(t_)

            def one(args):
                qi, start = args
                qpos = start + jnp.arange(Q_BLOCK)
                mask = qpos[:, None] >= kpos[None, :]
                return diff_mix(qi, (k_new,), (v_new,), (mask,), lam)

            out = lax.map(one, (qb, jnp.arange(nb) * Q_BLOCK))
            return jnp.moveaxis(out, 0, 1).reshape(bsz, t_, H_B, DV_B)
        return attend
    k_past, v_past = past

    def attend(q, lam):
        t_ = q.shape[1]
        causal = jnp.tril(jnp.ones((t_, t_), dtype=bool))
        return diff_mix(q, (k_past, k_new), (v_past, v_new), (None, causal), lam)
    return attend


def diff_layer(x, g, w_in, lq1, lk1, lq2, lk2, subln, w_out, pos, attend, lam_init):
    bsz, t_, _ = x.shape
    p = rmsnorm(x, g) @ w_in
    q = rope(p[..., :QK_B].reshape(bsz, t_, H_B, 2, DK_B), pos)
    z = p[..., QK_B:]
    lam = (jnp.exp(jnp.sum(lq1.astype(F32) * lk1.astype(F32)))
           - jnp.exp(jnp.sum(lq2.astype(F32) * lk2.astype(F32))) + lam_init)
    o = head_rmsnorm(attend(q, lam), subln) * (1.0 - lam_init)
    y = (o.reshape(bsz, t_, D_B).astype(x.dtype) * jax.nn.silu(z)) @ w_out
    return x + y


def trunk(x, pos, c_in, n_in, m_in, past, norm_g, w_in_a, b_i_a, b_f_a, hnorm_a, w_out_a,
          kv_norm_g, w_kv, w_in_b, lam_q1, lam_k1, lam_q2, lam_k2, subln_b, w_out_b, final_norm_g):
    cs, ns, ms = [], [], []
    k_new = v_new = attend = None
    for l in range(DEPTH):
        if l < N_A:
            x, c1, n1, m1 = mlstm_layer(x, norm_g[l], w_in_a[l], b_i_a[l], b_f_a[l], hnorm_a[l], w_out_a[l],
                                        c_in[l], n_in[l], m_in[l])
            cs.append(c1)
            ns.append(n1)
            ms.append(m1)
            if l == N_A - 1:
                k_new, v_new = shared_kv(x, kv_norm_g, w_kv, pos)
                attend = make_attend(k_new, v_new, past)
        else:
            j = l - N_A
            lam_init = 0.8 - 0.6 * math.exp(-0.3 * l)
            x = diff_layer(x, norm_g[l], w_in_b[j], lam_q1[j], lam_k1[j], lam_q2[j], lam_k2[j],
                           subln_b[j], w_out_b[j], pos, attend, lam_init)
    y = rmsnorm(x, final_norm_g)
    return y, k_new, v_new, jnp.stack(cs), jnp.stack(ns), jnp.stack(ms)


def setup_inputs(seed: int = 0) -> dict:
    key = jax.random.key(seed)
    ks = jax.random.split(key, 24)
    n_pages = PAST_LEN // PAGE_SIZE
    n_used = DEC_BATCH * n_pages
    n_pool = n_used + max(1, n_used // 4)
    nrm = lambda k, shape, s: jax.random.normal(k, shape, F32) * s
    return {
        'x_prompt': nrm(ks[0], (BATCH, SEQ, D_MODEL), 1.0),
        'x_sample': nrm(ks[1], (DEC_BATCH, DEC_SEQ, D_MODEL), 1.0),
        'cache_k': nrm(ks[2], (n_pool, PAGE_SIZE, H_B, 2, DK_B), 1.0),
        'cache_v': nrm(ks[3], (n_pool, PAGE_SIZE, H_B, DV_B), 1.0),
        'state_C': nrm(ks[4], (N_A, DEC_BATCH, H_A, DH_A, DH_A), 0.5),
        'state_n': nrm(ks[5], (N_A, DEC_BATCH, H_A, DH_A), 0.5),
        'state_m': nrm(ks[6], (N_A, DEC_BATCH, H_A), 1.0),
        'page_table': jax.random.permutation(ks[7], n_pool)[:n_used].reshape(DEC_BATCH, n_pages).astype(jnp.int32),
        'norm_g': 1.0 + nrm(ks[8], (DEPTH, D_MODEL), 0.02),
        'w_in_a': nrm(ks[9], (N_A, D_MODEL, A_IN), D_MODEL ** -0.5),
        'b_i_a': nrm(ks[10], (N_A, H_A), 0.1),
        'b_f_a': jnp.linspace(F_BIAS_LO, F_BIAS_HI, H_A, dtype=F32)[None, :] + nrm(ks[11], (N_A, H_A), 0.1),
        'hnorm_a': 1.0 + nrm(ks[12], (N_A, D_A), 0.02),
        'w_out_a': nrm(ks[13], (N_A, D_A, D_MODEL), D_A ** -0.5),
        'kv_norm_g': 1.0 + nrm(ks[14], (D_MODEL,), 0.02),
        'w_kv': nrm(ks[15], (D_MODEL, KV_OUT), D_MODEL ** -0.5),
        'w_in_b': nrm(ks[16], (N_B, D_MODEL, B_IN), D_MODEL ** -0.5),
        'lam_q1': nrm(ks[17], (N_B, DK_B), 0.1),
        'lam_k1': nrm(ks[18], (N_B, DK_B), 0.1),
        'lam_q2': nrm(ks[19], (N_B, DK_B), 0.1),
        'lam_k2': nrm(ks[20], (N_B, DK_B), 0.1),
        'subln_b': 1.0 + nrm(ks[21], (N_B, DV_B), 0.02),
        'w_out_b': nrm(ks[22], (N_B, D_B, D_MODEL), D_B ** -0.5),
        'final_norm_g': 1.0 + nrm(ks[23], (D_MODEL,), 0.02),
    }


def reference(x_prompt, x_sample, cache_k, cache_v, state_C, state_n, state_m, page_table,
              norm_g, w_in_a, b_i_a, b_f_a, hnorm_a, w_out_a, kv_norm_g, w_kv, w_in_b,
              lam_q1, lam_k1, lam_q2, lam_k2, subln_b, w_out_b, final_norm_g):
    bp, tp, _ = x_prompt.shape
    c0 = jnp.zeros((N_A, bp, H_A, DH_A, DH_A), F32)
    n0 = jnp.zeros((N_A, bp, H_A, DH_A), F32)
    m0 = jnp.zeros((N_A, bp, H_A), F32)
    pos_p = jnp.arange(tp, dtype=F32)
    y_p, k_p, v_p, c_p, n_p, m_p = trunk(
        x_prompt, pos_p, c0, n0, m0, None, norm_g, w_in_a, b_i_a, b_f_a, hnorm_a, w_out_a,
        kv_norm_g, w_kv, w_in_b, lam_q1, lam_k1, lam_q2, lam_k2, subln_b, w_out_b, final_norm_g)
    db, ts, _ = x_sample.shape
    past_len = page_table.shape[1] * PAGE_SIZE
    k_past = cache_k[page_table].reshape((db, past_len) + cache_k.shape[2:])
    v_past = cache_v[page_table].reshape((db, past_len) + cache_v.shape[2:])
    pos_s = past_len + jnp.arange(ts, dtype=F32)
    y_s, k_s, v_s, c_s, n_s, m_s = trunk(
        x_sample, pos_s, state_C, state_n, state_m, (k_past, v_past), norm_g, w_in_a, b_i_a, b_f_a,
        hnorm_a, w_out_a, kv_norm_g, w_kv, w_in_b, lam_q1, lam_k1, lam_q2, lam_k2, subln_b, w_out_b,
        final_norm_g)
    sd = state_C.dtype
    xd = x_prompt.dtype
    return (y_p, y_s, k_p, v_p, c_p.astype(xd), n_p.astype(xd), m_p.astype(xd),
            k_s, v_s, c_s.astype(sd), n_s.astype(sd), m_s.astype(sd))
```

```python
import functools
import math

import jax
import jax.numpy as jnp
from jax import lax
from jax.experimental import pallas as pl
from jax.experimental.pallas import tpu as pltpu

F32 = jnp.float32
BF16 = jnp.bfloat16

EPS = 1e-6
ROPE_THETA = 10000.0
LANES = 128
VMEM_LIMIT_BYTES = 56 * 1024 * 1024
NEG_INF = float("-inf")


def _dot(a, b):
    return jnp.dot(a, b, preferred_element_type=F32)


def _dot_nt(a, b):
    return lax.dot_general(a, b, (((1,), (1,)), ((), ())), preferred_element_type=F32)


def _dot_tn(a, b):
    return lax.dot_general(a, b, (((0,), (0,)), ((), ())), preferred_element_type=F32)


def _rmsnorm_rows(x, g):
    return x * lax.rsqrt(jnp.mean(x * x, axis=-1, keepdims=True) + EPS) * g


def _sigmoid(x):
    return 1.0 / (1.0 + jnp.exp(-x))


def _log_sigmoid(x):
    return jnp.minimum(x, 0.0) - jnp.log1p(jnp.exp(-jnp.abs(x)))


def _mlstm_layer_kernel(x_ref, g_ref, w_ref, wgt_ref, wgc_ref, bgt_ref, bgc_ref, hn_ref, wo_ref,
                        c0_ref, n0_ref, m0_ref,
                        y_ref, c_ref, n_ref, m_ref,
                        caug_ref, msc_ref, gbuf_ref,
                        *, n_heads, chunk, t_valid):
    sb, tbs, d = x_ref.shape
    dh = caug_ref.shape[2]
    da = n_heads * dh
    L = chunk
    cps = tbs // L
    t = pl.program_id(1)
    nt = pl.num_programs(1)

    row_i = lax.broadcasted_iota(jnp.int32, (dh, dh), 0)

    @pl.when(t == 0)
    def _init():
        for s in range(sb):
            for h in range(n_heads):
                npad = jnp.where(row_i == 0, jnp.broadcast_to(n0_ref[s, h], (dh, dh)), 0.0)
                caug_ref[s, h] = jnp.concatenate([c0_ref[s, h], npad], axis=0).T
                msc_ref[s, h] = jnp.broadcast_to(m0_ref[s, h], msc_ref.shape[2:])

    x = x_ref[...].reshape(sb * tbs, d)
    xb = _rmsnorm_rows(x, g_ref[...]).astype(BF16)

    gt = _dot_nt(wgt_ref[...], xb) + bgt_ref[...]
    gc = _dot(xb, wgc_ref[...]) + bgc_ref[...]
    qb = _dot(xb, w_ref[:, 0:da]).astype(BF16)
    kb = (_dot(xb, w_ref[:, da:2 * da]) * (dh ** -0.5)).astype(BF16)
    v = _dot(xb, w_ref[:, 2 * da:3 * da])
    o = _dot(xb, w_ref[:, 3 * da:4 * da])
    z = _dot(xb, w_ref[:, 4 * da:5 * da])

    jj = lax.broadcasted_iota(jnp.int32, (L, L), 0)
    ss = lax.broadcasted_iota(jnp.int32, (L, L), 1)
    tril = jj >= ss
    triu = jj <= ss
    lane_l = lax.broadcasted_iota(jnp.int32, (1, L), 1)
    row_l = lax.broadcasted_iota(jnp.int32, (L, 1), 0)
    ones_col = (lax.broadcasted_iota(jnp.int32, (L, dh), 1) == 0).astype(F32)

    for c in range(sb * cps):
        s = c // cps
        r0 = c * L
        tok0 = t * tbs + (c % cps) * L
        i_rows = gt[0:n_heads, r0:r0 + L]
        f_rows = _log_sigmoid(gt[n_heads:2 * n_heads, r0:r0 + L])
        i_cols = gc[r0:r0 + L, 0:n_heads]
        f_cols = _log_sigmoid(gc[r0:r0 + L, n_heads:2 * n_heads])
        if t_valid is not None:
            ok_r = (tok0 + lane_l) < t_valid
            ok_c = (tok0 + row_l) < t_valid
            i_rows = jnp.where(ok_r, i_rows, NEG_INF)
            f_rows = jnp.where(ok_r, f_rows, 0.0)
            i_cols = jnp.where(ok_c, i_cols, NEG_INF)
            f_cols = jnp.where(ok_c, f_cols, 0.0)
        for h in range(n_heads):
            cols = slice(h * dh, (h + 1) * dh)
            rows = slice(r0, r0 + L)
            lf_row = f_rows[h:h + 1, :]
            lf_col = f_cols[:, h:h + 1]
            ic_row = i_rows[h:h + 1, :]
            ic_col = i_cols[:, h:h + 1]
            m_prev = msc_ref[s, h][0:1, 0:1]
            bc = jnp.sum(jnp.where(tril, lf_row, 0.0), axis=1, keepdims=True)
            br = jnp.sum(jnp.where(triu, lf_col, 0.0), axis=0, keepdims=True)
            b_last = bc[L - 1:L, :]
            rrow = ic_row - br
            dmat = jnp.where(tril, bc + rrow, NEG_INF)
            inter = bc + m_prev
            mj = jnp.maximum(inter, jnp.max(dmat, axis=1, keepdims=True))
            w_inter = jnp.exp(inter - mj)
            m_new = jnp.maximum(b_last + m_prev, jnp.max(b_last + rrow, axis=1, keepdims=True))
            decay = jnp.exp(b_last + m_prev - m_new)
            wg_col = jnp.exp(b_last - bc + ic_col - m_new)

            qh = qb[rows, cols]
            kh = kb[rows, cols]
            vh = v[rows, cols]
            smat = _dot_nt(qh, kh)
            a = jnp.exp(dmat - mj) * smat
            vaug = jnp.concatenate([vh, ones_col], axis=1)
            intra = _dot(a.astype(BF16), vaug.astype(BF16))
            cst = caug_ref[s, h]
            comb = w_inter * _dot(qh, cst.astype(BF16)) + intra
            num = comb[:, 0:dh]
            den = comb[:, dh:dh + 1]
            hcell = num / jnp.maximum(jnp.abs(den), jnp.exp(-mj))
            ho = _sigmoid(o[rows, cols]) * hcell
            hnrm = ho * lax.rsqrt(jnp.mean(ho * ho, axis=-1, keepdims=True) + EPS) * hn_ref[:, cols]
            zz = z[rows, cols]
            gbuf_ref[rows, cols] = (hnrm * (zz * _sigmoid(zz))).astype(BF16)

            upd = _dot_tn(kh, (wg_col * vaug).astype(BF16))
            caug_ref[s, h] = decay * cst + upd
            msc_ref[s, h] = jnp.broadcast_to(m_new, msc_ref.shape[2:])

    y = _dot(gbuf_ref[...], wo_ref[...]) + x
    y_ref[...] = y.reshape(sb, tbs, d)

    @pl.when(t == nt - 1)
    def _final():
        for s in range(sb):
            for h in range(n_heads):
                ct = caug_ref[s, h].T
                c_ref[s, h] = ct[0:dh]
                n_ref[s, h] = ct[dh:dh + 1]
                m_ref[s, h] = msc_ref[s, h][0:1, 0:1]


def _mlstm_layer(x, g, w_in, b_i, b_f, hnorm, w_out, c0, n0, m0, *, seqs_per_block, tokens_per_block,
                 chunk, t_valid):
    b, tp, d = x.shape
    nh = b_i.shape[-1]
    da = w_out.shape[0]
    dh = da // nh
    sb, tbs = seqs_per_block, tokens_per_block
    assert b % sb == 0 and tp % tbs == 0 and tbs % chunk == 0
    assert sb == 1 or tbs == tp
    w_main = w_in[:, :5 * da].astype(BF16)
    w_gate = w_in[:, 5 * da:5 * da + 2 * nh]
    wgt = w_gate.T.astype(BF16)
    wgc = jnp.pad(w_gate, ((0, 0), (0, LANES - 2 * nh))).astype(BF16)
    bias = jnp.concatenate([b_i, b_f]).astype(F32)
    bgt = bias.reshape(2 * nh, 1)
    bgc = jnp.pad(bias, (0, LANES - 2 * nh)).reshape(1, LANES)
    full = lambda shape: pl.BlockSpec(shape, lambda i, j: (0,) * len(shape))
    kern = functools.partial(_mlstm_layer_kernel, n_heads=nh, chunk=chunk,
                             t_valid=None if t_valid == tp else t_valid)
    y, c1, n1, m1 = pl.pallas_call(
        kern,
        grid=(b // sb, tp // tbs),
        in_specs=[
            pl.BlockSpec((sb, tbs, d), lambda i, j: (i, j, 0)),
            full((1, d)), full((d, 5 * da)), full((2 * nh, d)), full((d, LANES)),
            full((2 * nh, 1)), full((1, LANES)), full((1, da)), full((da, d)),
            pl.BlockSpec((sb, nh, dh, dh), lambda i, j: (i, 0, 0, 0)),
            pl.BlockSpec((sb, nh, 1, dh), lambda i, j: (i, 0, 0, 0)),
            pl.BlockSpec((sb, nh, 1, 1), lambda i, j: (i, 0, 0, 0)),
        ],
        out_specs=[
            pl.BlockSpec((sb, tbs, d), lambda i, j: (i, j, 0)),
            pl.BlockSpec((sb, nh, dh, dh), lambda i, j: (i, 0, 0, 0)),
            pl.BlockSpec((sb, nh, 1, dh), lambda i, j: (i, 0, 0, 0)),
            pl.BlockSpec((sb, nh, 1, 1), lambda i, j: (i, 0, 0, 0)),
        ],
        out_shape=[
            jax.ShapeDtypeStruct((b, tp, d), F32),
            jax.ShapeDtypeStruct((b, nh, dh, dh), F32),
            jax.ShapeDtypeStruct((b, nh, 1, dh), F32),
            jax.ShapeDtypeStruct((b, nh, 1, 1), F32),
        ],
        scratch_shapes=[
            pltpu.VMEM((sb, nh, dh, 2 * dh), F32),
            pltpu.VMEM((sb, nh, 8, LANES), F32),
            pltpu.VMEM((sb * tbs, da), BF16),
        ],
        compiler_params=pltpu.CompilerParams(dimension_semantics=("arbitrary", "arbitrary"),
                                             vmem_limit_bytes=VMEM_LIMIT_BYTES),
        name="mlstm_layer",
    )(x, g.reshape(1, d), w_main, wgt, wgc, bgt, bgc, hnorm.reshape(1, da), w_out.astype(BF16),
      c0.astype(F32), n0.astype(F32).reshape(b, nh, 1, dh), m0.astype(F32).reshape(b, nh, 1, 1))
    return y, c1, n1.reshape(b, nh, dh), m1.reshape(b, nh)


def _rope_tables(pos, dk):
    half = dk // 2
    inv = ROPE_THETA ** (-(jnp.arange(half, dtype=F32) / half))
    ang = pos[:, None] * inv[None, :]
    cos, sin = jnp.cos(ang), jnp.sin(ang)
    reps = LANES // dk
    return (jnp.tile(jnp.concatenate([cos, cos], axis=-1), (1, reps)),
            jnp.tile(jnp.concatenate([-sin, sin], axis=-1), (1, reps)))


def _norm_proj_rope_kernel(x_ref, g_ref, w_ref, cos_ref, sin_ref, *out_refs, dk, rot_scale, mode):
    xb = _rmsnorm_rows(x_ref[...], g_ref[...]).astype(BF16)
    p = _dot(xb, w_ref[...])
    dr = p.shape[1] // 2
    a, rest = p[:, :dr], p[:, dr:]
    half = dk // 2
    reps = dr // LANES
    cos = jnp.tile(cos_ref[...], (1, reps))
    sin = jnp.tile(sin_ref[...], (1, reps))
    lane = lax.broadcasted_iota(jnp.int32, a.shape, 1)
    first = (lane & (dk - 1)) < half
    partner = jnp.where(first, pltpu.roll(a, dr - half, 1), pltpu.roll(a, half, 1))
    rot = a * cos + partner * sin
    if mode == "kv":
        k_ref, v_ref, kb_ref, vb_ref = out_refs
        k_ref[...] = rot
        v_ref[...] = rest
        kb_ref[...] = rot.astype(BF16)
        vb_ref[...] = rest.astype(BF16)
    else:
        qb_ref, z_ref = out_refs
        qb_ref[...] = (rot * rot_scale).astype(BF16)
        z_ref[...] = rest


def _norm_proj_rope(x, g, w, cos_t, sin_t, *, dk, mode, rot_scale=1.0, rows_per_block):
    n, d = x.shape
    nout = w.shape[1]
    dr = nout // 2
    tm = rows_per_block
    nblk_t = cos_t.shape[0] // tm
    assert n % tm == 0 and cos_t.shape[0] % tm == 0 and (dk & (dk - 1)) == 0
    row = lambda cols: pl.BlockSpec((tm, cols), lambda i: (i, 0))
    full = lambda shape: pl.BlockSpec(shape, lambda i: (0,) * len(shape))
    tab = pl.BlockSpec((tm, LANES), lambda i: (i % nblk_t, 0))
    if mode == "kv":
        out_specs = [row(dr), row(nout - dr), row(dr), row(nout - dr)]
        out_shape = [jax.ShapeDtypeStruct((n, dr), F32), jax.ShapeDtypeStruct((n, nout - dr), F32),
                     jax.ShapeDtypeStruct((n, dr), BF16), jax.ShapeDtypeStruct((n, nout - dr), BF16)]
    else:
        out_specs = [row(dr), row(nout - dr)]
        out_shape = [jax.ShapeDtypeStruct((n, dr), BF16), jax.ShapeDtypeStruct((n, nout - dr), F32)]
    return pl.pallas_call(
        functools.partial(_norm_proj_rope_kernel, dk=dk, rot_scale=rot_scale, mode=mode),
        grid=(n // tm,),
        in_specs=[row(d), full((1, d)), full((d, nout)), tab, tab],
        out_specs=out_specs,
        out_shape=out_shape,
        compiler_params=pltpu.CompilerParams(dimension_semantics=("arbitrary",),
                                             vmem_limit_bytes=VMEM_LIMIT_BYTES),
        name="norm_proj_rope_" + mode,
    )(x, g.reshape(1, d), w.astype(BF16), cos_t, sin_t)


def _lambda(lamp_ref, lam_init):
    lp = lamp_ref[...]
    s1 = jnp.sum(lp[0:1] * lp[1:2], axis=1, keepdims=True)
    s2 = jnp.sum(lp[2:3] * lp[3:4], axis=1, keepdims=True)
    return jnp.exp(s1) - jnp.exp(s2) + lam_init


def _split_subheads(q, dk):
    lane = lax.broadcasted_iota(jnp.int32, q.shape, 1)
    zero = jnp.zeros_like(q)
    return jnp.concatenate([jnp.where(lane < dk, q, zero), jnp.where(lane >= dk, q, zero)], axis=0)


def _softmax_step(s, v, m_ref, acc_ref):
    dv = v.shape[1]
    ones_col = (lax.broadcasted_iota(jnp.int32, v.shape, 1) == 0).astype(v.dtype)
    m_prev = m_ref[...]
    m_new = jnp.maximum(m_prev, jnp.max(s, axis=1, keepdims=True))
    alpha = jnp.exp(m_prev - m_new)
    p = jnp.exp(s - m_new).astype(BF16)
    acc_ref[...] = alpha * acc_ref[...] + _dot(p, jnp.concatenate([v, ones_col], axis=1))
    m_ref[...] = m_new


def _diff_finish(acc, tq, dv, lam, sub, z, lam_init):
    o_all = acc[:, 0:dv] / acc[:, dv:dv + 1]
    o = o_all[0:tq] - lam * o_all[tq:2 * tq]
    on = o * lax.rsqrt(jnp.mean(o * o, axis=-1, keepdims=True) + EPS) * sub * (1.0 - lam_init)
    return (on * (z * _sigmoid(z))).astype(BF16)


def _flash_diff_attn_kernel(q_ref, k_ref, v_ref, z_ref, sub_ref, lamp_ref, g_ref, m_ref, acc_ref,
                            *, dk, lam_init):
    qi = pl.program_id(2)
    tq = q_ref.shape[1]
    dv = v_ref.shape[2]
    tk = tq
    q2 = _split_subheads(q_ref[0], dk)
    m_ref[...] = jnp.full(m_ref.shape, NEG_INF, F32)
    acc_ref[...] = jnp.zeros(acc_ref.shape, F32)

    def full_block(kb, carry):
        k0 = pl.multiple_of(kb * tk, tk)
        s = _dot_nt(q2, k_ref[0, pl.ds(k0, tk), :])
        _softmax_step(s, v_ref[0, pl.ds(k0, tk), :], m_ref, acc_ref)
        return carry

    lax.fori_loop(0, qi, full_block, 0)

    k0 = pl.multiple_of(qi * tk, tk)
    s = _dot_nt(q2, k_ref[0, pl.ds(k0, tk), :])
    qpos = lax.broadcasted_iota(jnp.int32, s.shape, 0)
    qpos = jnp.where(qpos >= tq, qpos - tq, qpos)
    kpos = lax.broadcasted_iota(jnp.int32, s.shape, 1)
    s = jnp.where(qpos >= kpos, s, NEG_INF)
    _softmax_step(s, v_ref[0, pl.ds(k0, tk), :], m_ref, acc_ref)

    lam = _lambda(lamp_ref, lam_init)
    g_ref[0] = _diff_finish(acc_ref[...], tq, dv, lam, sub_ref[...], z_ref[0], lam_init)


def _flash_diff_attn(qb, kb, vb, z, subln, lamp, *, dk, lam_init, q_block):
    b, t, qk = qb.shape
    dv = subln.shape[-1]
    nh = qk // (2 * dk)
    tq = q_block
    assert t % tq == 0 and 2 * dk == LANES and dv == LANES
    qspec = pl.BlockSpec((1, tq, dv), lambda i, h, j: (i, j, h))
    kvspec = pl.BlockSpec((1, t, dv), lambda i, h, j: (i, 0, h))
    full = lambda shape: pl.BlockSpec(shape, lambda i, h, j: (0,) * len(shape))
    return pl.pallas_call(
        functools.partial(_flash_diff_attn_kernel, dk=dk, lam_init=lam_init),
        grid=(b, nh, t // tq),
        in_specs=[qspec, kvspec, kvspec, qspec, full((1, dv)), full(lamp.shape)],
        out_specs=qspec,
        out_shape=jax.ShapeDtypeStruct((b, t, nh * dv), BF16),
        scratch_shapes=[pltpu.VMEM((2 * tq, 1), F32), pltpu.VMEM((2 * tq, 2 * dv), F32)],
        compiler_params=pltpu.CompilerParams(dimension_semantics=("arbitrary",) * 3,
                                             vmem_limit_bytes=VMEM_LIMIT_BYTES),
        name="flash_diff_attn",
    )(qb, kb, vb, z, subln.reshape(1, dv), lamp)


def _paged_diff_attn_kernel(pt_ref, q_ref, *refs, pages_per_step, n_heads, dk, t_valid, lam_init):
    del pt_ref
    pb = pages_per_step
    kp_refs, vp_refs = refs[:pb], refs[pb:2 * pb]
    kn_ref, vn_ref, z_ref, sub_ref, lamp_ref, g_ref, m_ref, acc_ref = refs[2 * pb:]
    step = pl.program_id(1)
    tq = q_ref.shape[1]
    dv = sub_ref.shape[1]

    @pl.when(step == 0)
    def _init():
        m_ref[...] = jnp.full(m_ref.shape, NEG_INF, F32)
        acc_ref[...] = jnp.zeros(acc_ref.shape, F32)

    kblk = jnp.concatenate([r[0].astype(BF16) for r in kp_refs], axis=0)
    vblk = jnp.concatenate([r[0].astype(BF16) for r in vp_refs], axis=0)
    q = q_ref[0]
    for h in range(n_heads):
        cols = slice(h * dv, (h + 1) * dv)
        q2 = _split_subheads(q[:, cols], dk)
        _softmax_step(_dot_nt(q2, kblk[:, cols]), vblk[:, cols], m_ref.at[h], acc_ref.at[h])

    @pl.when(step == pl.num_programs(1) - 1)
    def _final():
        lam = _lambda(lamp_ref, lam_init)
        kn, vn, z = kn_ref[0], vn_ref[0], z_ref[0]
        for h in range(n_heads):
            cols = slice(h * dv, (h + 1) * dv)
            q2 = _split_subheads(q[:, cols], dk)
            s = _dot_nt(q2, kn[:, cols])
            qpos = lax.broadcasted_iota(jnp.int32, s.shape, 0)
            qpos = jnp.where(qpos >= tq, qpos - tq, qpos)
            kpos = lax.broadcasted_iota(jnp.int32, s.shape, 1)
            s = jnp.where((kpos <= qpos) & (kpos < t_valid), s, NEG_INF)
            _softmax_step(s, vn[:, cols], m_ref.at[h], acc_ref.at[h])
            g_ref[0, :, cols] = _diff_finish(acc_ref[h], tq, dv, lam, sub_ref[...], z[:, cols], lam_init)


def _paged_diff_attn(qb, cache_k, cache_v, page_table, kn, vn, z, subln, lamp, *, dk, t_valid, lam_init,
                     pages_per_step):
    db, tq, width = qb.shape
    dv = subln.shape[-1]
    nh = width // dv
    page = cache_k.shape[1]
    n_pages = page_table.shape[1]
    pb = pages_per_step
    assert n_pages % pb == 0 and 2 * dk == dv

    def page_spec(j):
        return pl.BlockSpec((1, page, width), lambda i, s, pt: (pt[i, s * pb + j], 0, 0))

    tok = pl.BlockSpec((1, tq, width), lambda i, s, pt: (i, 0, 0))
    full = lambda shape: pl.BlockSpec(shape, lambda i, s, pt: (0,) * len(shape))
    grid_spec = pltpu.PrefetchScalarGridSpec(
        num_scalar_prefetch=1,
        grid=(db, n_pages // pb),
        in_specs=[tok] + [page_spec(j) for j in range(pb)] * 2 + [tok, tok, tok, full((1, dv)), full(lamp.shape)],
        out_specs=tok,
        scratch_shapes=[pltpu.VMEM((nh, 2 * tq, 1), F32), pltpu.VMEM((nh, 2 * tq, 2 * dv), F32)],
    )
    return pl.pallas_call(
        functools.partial(_paged_diff_attn_kernel, pages_per_step=pb, n_heads=nh, dk=dk, t_valid=t_valid,
                          lam_init=lam_init),
        grid_spec=grid_spec,
        out_shape=jax.ShapeDtypeStruct((db, tq, width), BF16),
        compiler_params=pltpu.CompilerParams(dimension_semantics=("arbitrary", "arbitrary"),
                                             vmem_limit_bytes=VMEM_LIMIT_BYTES),
        name="paged_diff_attn",
    )(page_table, qb, *([cache_k] * pb), *([cache_v] * pb), kn, vn, z, subln.reshape(1, dv), lamp)


def _out_proj_kernel(g_ref, w_ref, x_ref, gf_ref, y_ref, *, final):
    y = _dot(g_ref[...], w_ref[...]) + x_ref[...]
    if final:
        y = _rmsnorm_rows(y, gf_ref[...])
    y_ref[...] = y


def _out_proj(g, w, x, gf, *, final, rows_per_block):
    n, d = x.shape
    tm = rows_per_block
    assert n % tm == 0
    row = lambda cols: pl.BlockSpec((tm, cols), lambda i: (i, 0))
    full = lambda shape: pl.BlockSpec(shape, lambda i: (0,) * len(shape))
    return pl.pallas_call(
        functools.partial(_out_proj_kernel, final=final),
        grid=(n // tm,),
        in_specs=[row(g.shape[1]), full(w.shape), row(d), full((1, d))],
        out_specs=row(d),
        out_shape=jax.ShapeDtypeStruct((n, d), F32),
        compiler_params=pltpu.CompilerParams(dimension_semantics=("arbitrary",),
                                             vmem_limit_bytes=VMEM_LIMIT_BYTES),
        name="out_proj",
    )(g, w.astype(BF16), x, gf.reshape(1, d))


PROMPT_ROWS = 256
PROMPT_CHUNK = 128
PROMPT_Q_BLOCK = 512
SAMPLE_SEQ_PAD = 16
SAMPLE_SEQS_PER_BLOCK = 8
SAMPLE_Q_PAD = 8
PAGES_PER_STEP = 8


def _largest_divisor(n, cap):
    return max(k for k in range(1, cap + 1) if n % k == 0)


def kernel(x_prompt, x_sample, cache_k, cache_v, state_C, state_n, state_m, page_table, norm_g, w_in_a, b_i_a, b_f_a, hnorm_a, w_out_a, kv_norm_g, w_kv, w_in_b, lam_q1, lam_k1, lam_q2, lam_k2, subln_b, w_out_b, final_norm_g):
    n_a, n_b = w_in_a.shape[0], w_in_b.shape[0]
    depth = n_a + n_b
    nh_a = b_i_a.shape[-1]
    dh_a = w_out_a.shape[1] // nh_a
    dk = lam_q1.shape[-1]
    attn_scale = dk ** -0.5
    lamp = jnp.stack([lam_q1, lam_k1, lam_q2, lam_k2], axis=1).astype(F32)
    lam_inits = [0.8 - 0.6 * math.exp(-0.3 * (n_a + j)) for j in range(n_b)]

    def diff_layers(xf, cos_t, sin_t, rows, attend):
        for j in range(n_b):
            qb, z = _norm_proj_rope(xf, norm_g[n_a + j], w_in_b[j], cos_t, sin_t, dk=dk, mode="q",
                                    rot_scale=attn_scale, rows_per_block=rows)
            g = attend(qb, z, j)
            xf = _out_proj(g, w_out_b[j], xf, final_norm_g, final=(n_a + j == depth - 1), rows_per_block=rows)
        return xf

    bp, tp, d = x_prompt.shape
    rows_p = _largest_divisor(tp, PROMPT_ROWS)
    chunk_p = _largest_divisor(rows_p, PROMPT_CHUNK)
    x = x_prompt
    cs, ns, ms = [], [], []
    for l in range(n_a):
        x, c1, n1, m1 = _mlstm_layer(
            x, norm_g[l], w_in_a[l], b_i_a[l], b_f_a[l], hnorm_a[l], w_out_a[l],
            jnp.zeros((bp, nh_a, dh_a, dh_a), F32), jnp.zeros((bp, nh_a, dh_a), F32), jnp.zeros((bp, nh_a), F32),
            seqs_per_block=1, tokens_per_block=rows_p, chunk=chunk_p, t_valid=tp)
        cs.append(c1)
        ns.append(n1)
        ms.append(m1)
    cos_p, sin_p = _rope_tables(jnp.arange(tp, dtype=F32), dk)
    xf = x.reshape(bp * tp, d)
    k_p, v_p, kb, vb = _norm_proj_rope(xf, kv_norm_g, w_kv, cos_p, sin_p, dk=dk, mode="kv", rows_per_block=rows_p)
    kb3, vb3 = kb.reshape(bp, tp, -1), vb.reshape(bp, tp, -1)
    q_block = _largest_divisor(tp, PROMPT_Q_BLOCK)

    def attend_p(qb, z, j):
        g = _flash_diff_attn(qb.reshape(bp, tp, -1), kb3, vb3, z.reshape(bp, tp, -1), subln_b[j], lamp[j],
                             dk=dk, lam_init=lam_inits[j], q_block=q_block)
        return g.reshape(bp * tp, -1)

    y_p = diff_layers(xf, cos_p, sin_p, rows_p, attend_p).reshape(bp, tp, d)
    xd = x_prompt.dtype
    c_p, n_p, m_p = jnp.stack(cs).astype(xd), jnp.stack(ns).astype(xd), jnp.stack(ms).astype(xd)

    db, ts, _ = x_sample.shape
    n_pages = page_table.shape[1]
    page = cache_k.shape[1]
    past_len = n_pages * page
    x = jnp.pad(x_sample, ((0, 0), (0, SAMPLE_SEQ_PAD - ts), (0, 0)))
    sb = _largest_divisor(db, SAMPLE_SEQS_PER_BLOCK)
    cs, ns, ms = [], [], []
    for l in range(n_a):
        x, c1, n1, m1 = _mlstm_layer(
            x, norm_g[l], w_in_a[l], b_i_a[l], b_f_a[l], hnorm_a[l], w_out_a[l],
            state_C[l], state_n[l], state_m[l],
            seqs_per_block=sb, tokens_per_block=SAMPLE_SEQ_PAD, chunk=SAMPLE_SEQ_PAD, t_valid=ts)
        cs.append(c1)
        ns.append(n1)
        ms.append(m1)
    xf = x[:, :ts].reshape(db * ts, d)
    pos_s = jnp.tile(past_len + jnp.arange(ts, dtype=F32), db)
    cos_s, sin_s = _rope_tables(pos_s, dk)
    rows_s = db * ts
    k_s, v_s, kbs, vbs = _norm_proj_rope(xf, kv_norm_g, w_kv, cos_s, sin_s, dk=dk, mode="kv", rows_per_block=rows_s)
    pad_q = lambda a: jnp.pad(a.reshape(db, ts, -1), ((0, 0), (0, SAMPLE_Q_PAD - ts), (0, 0)))
    kn, vn = pad_q(kbs), pad_q(vbs)
    ck = cache_k.reshape(cache_k.shape[0], page, -1)
    cv = cache_v.reshape(cache_v.shape[0], page, -1)
    pps = _largest_divisor(n_pages, PAGES_PER_STEP)

    def attend_s(qb, z, j):
        g = _paged_diff_attn(pad_q(qb), ck, cv, page_table, kn, vn, pad_q(z), subln_b[j], lamp[j],
                             dk=dk, t_valid=ts, lam_init=lam_inits[j], pages_per_step=pps)
        return g[:, :ts].reshape(db * ts, -1)

    y_s = diff_layers(xf, cos_s, sin_s, rows_s, attend_s).reshape(db, ts, d)
    sd = state_C.dtype
    c_s, n_s, m_s = jnp.stack(cs).astype(sd), jnp.stack(ns).astype(sd), jnp.stack(ms).astype(sd)

    return (y_p, y_s,
            k_p.reshape(bp, tp, -1, 2, dk), v_p.reshape(bp, tp, -1, 2 * dk), c_p, n_p, m_p,
            k_s.reshape(db, ts, -1, 2, dk), v_s.reshape(db, ts, -1, 2 * dk), c_s, n_s, m_s)
```

```python
import functools
import math

import jax
import jax.numpy as jnp
from jax import lax
from jax.experimental import pallas as pl
from jax.experimental.pallas import tpu as pltpu

F32 = jnp.float32
BF16 = jnp.bfloat16

EPS = 1e-6
ROPE_THETA = 10000.0
LANES = 128
VMEM_LIMIT_BYTES = 56 * 1024 * 1024
NEG_INF = float("-inf")


def _dot(a, b):
    return jnp.dot(a, b, preferred_element_type=F32)


def _dot_nt(a, b):
    return lax.dot_general(a, b, (((1,), (1,)), ((), ())), preferred_element_type=F32)


def _dot_tn(a, b):
    return lax.dot_general(a, b, (((0,), (0,)), ((), ())), preferred_element_type=F32)


def _rmsnorm_rows(x, g):
    return x * lax.rsqrt(jnp.mean(x * x, axis=-1, keepdims=True) + EPS) * g


def _sigmoid(x):
    return 1.0 / (1.0 + jnp.exp(-x))


def _log_sigmoid(x):
    return jnp.minimum(x, 0.0) - jnp.log1p(jnp.exp(-jnp.abs(x)))


def _mlstm_layer_kernel(x_ref, g_ref, w_ref, wgt_ref, wgc_ref, bgt_ref, bgc_ref, hn_ref, wo_ref,
                        c0_ref, n0_ref, m0_ref,
                        y_ref, c_ref, n_ref, m_ref,
                        caug_ref, msc_ref, gbuf_ref,
                        *, n_heads, chunk, t_valid):
    sb, tbs, d = x_ref.shape
    dh = caug_ref.shape[2]
    da = n_heads * dh
    L = chunk
    cps = tbs // L
    t = pl.program_id(1)
    nt = pl.num_programs(1)

    row_i = lax.broadcasted_iota(jnp.int32, (dh, dh), 0)

    @pl.when(t == 0)
    def _init():
        for s in range(sb):
            for h in range(n_heads):
                npad = jnp.where(row_i == 0, jnp.broadcast_to(n0_ref[s, h], (dh, dh)), 0.0)
                caug_ref[s, h] = jnp.concatenate([c0_ref[s, h], npad], axis=0).T
                msc_ref[s, h] = jnp.broadcast_to(m0_ref[s, h], msc_ref.shape[2:])

    x = x_ref[...].reshape(sb * tbs, d)
    xb = _rmsnorm_rows(x, g_ref[...]).astype(BF16)

    gt = _dot_nt(wgt_ref[...], xb) + bgt_ref[...]
    gc = _dot(xb, wgc_ref[...]) + bgc_ref[...]
    qb = _dot(xb, w_ref[:, 0:da]).astype(BF16)
    kb = (_dot(xb, w_ref[:, da:2 * da]) * (dh ** -0.5)).astype(BF16)
    v = _dot(xb, w_ref[:, 2 * da:3 * da])
    o = _dot(xb, w_ref[:, 3 * da:4 * da])
    z = _dot(xb, w_ref[:, 4 * da:5 * da])

    jj = lax.broadcasted_iota(jnp.int32, (L, L), 0)
    ss = lax.broadcasted_iota(jnp.int32, (L, L), 1)
    tril = jj >= ss
    triu = jj <= ss
    lane_l = lax.broadcasted_iota(jnp.int32, (1, L), 1)
    row_l = lax.broadcasted_iota(jnp.int32, (L, 1), 0)
    ones_col = (lax.broadcasted_iota(jnp.int32, (L, dh), 1) == 0).astype(F32)

    for c in range(sb * cps):
        s = c // cps
        r0 = c * L
        tok0 = t * tbs + (c % cps) * L
        i_rows = gt[0:n_heads, r0:r0 + L]
        f_rows = _log_sigmoid(gt[n_heads:2 * n_heads, r0:r0 + L])
        i_cols = gc[r0:r0 + L, 0:n_heads]
        f_cols = _log_sigmoid(gc[r0:r0 + L, n_heads:2 * n_heads])
        if t_valid is not None:
            ok_r = (tok0 + lane_l) < t_valid
            ok_c = (tok0 + row_l) < t_valid
            i_rows = jnp.where(ok_r, i_rows, NEG_INF)
            f_rows = jnp.where(ok_r, f_rows, 0.0)
            i_cols = jnp.where(ok_c, i_cols, NEG_INF)
            f_cols = jnp.where(ok_c, f_cols, 0.0)
        for h in range(n_heads):
            cols = slice(h * dh, (h + 1) * dh)
            rows = slice(r0, r0 + L)
            lf_row = f_rows[h:h + 1, :]
            lf_col = f_cols[:, h:h + 1]
            ic_row = i_rows[h:h + 1, :]
            ic_col = i_cols[:, h:h + 1]
            m_prev = msc_ref[s, h][0:1, 0:1]
            bc = jnp.sum(jnp.where(tril, lf_row, 0.0), axis=1, keepdims=True)
            br = jnp.sum(jnp.where(triu, lf_col, 0.0), axis=0, keepdims=True)
            b_last = bc[L - 1:L, :]
            rrow = ic_row - br
            dmat = jnp.where(tril, bc + rrow, NEG_INF)
            inter = bc + m_prev
            mj = jnp.maximum(inter, jnp.max(dmat, axis=1, keepdims=True))
            w_inter = jnp.exp(inter - mj)
            m_new = jnp.maximum(b_last + m_prev, jnp.max(b_last + rrow, axis=1, keepdims=True))
            decay = jnp.exp(b_last + m_prev - m_new)
            wg_col = jnp.exp(b_last - bc + ic_col - m_new)

            qh = qb[rows, cols]
            kh = kb[rows, cols]
            vh = v[rows, cols]
            smat = _dot_nt(qh, kh)
            a = jnp.exp(dmat - mj) * smat
            vaug = jnp.concatenate([vh, ones_col], axis=1)
            intra = _dot(a.astype(BF16), vaug.astype(BF16))
            cst = caug_ref[s, h]
            comb = w_inter * _dot(qh, cst.astype(BF16)) + intra
            num = comb[:, 0:dh]
            den = comb[:, dh:dh + 1]
            hcell = num / jnp.maximum(jnp.abs(den), jnp.exp(-mj))
            ho = _sigmoid(o[rows, cols]) * hcell
            hnrm = ho * lax.rsqrt(jnp.mean(ho * ho, axis=-1, keepdims=True) + EPS) * hn_ref[:, cols]
            zz = z[rows, cols]
            gbuf_ref[rows, cols] = (hnrm * (zz * _sigmoid(zz))).astype(BF16)

            upd = _dot_tn(kh, (wg_col * vaug).astype(BF16))
            caug_ref[s, h] = decay * cst + upd
            msc_ref[s, h] = jnp.broadcast_to(m_new, msc_ref.shape[2:])

    y = _dot(gbuf_ref[...], wo_ref[...]) + x
    y_ref[...] = y.reshape(sb, tbs, d)

    @pl.when(t == nt - 1)
    def _final():
        for s in range(sb):
            for h in range(n_heads):
                ct = caug_ref[s, h].T
                c_ref[s, h] = ct[0:dh]
                n_ref[s, h] = ct[dh:dh + 1]
                m_ref[s, h] = msc_ref[s, h][0:1, 0:1]


def _mlstm_layer(x, g, w_in, b_i, b_f, hnorm, w_out, c0, n0, m0, *, seqs_per_block, tokens_per_block,
                 chunk, t_valid):
    b, tp, d = x.shape
    nh = b_i.shape[-1]
    da = w_out.shape[0]
    dh = da // nh
    sb, tbs = seqs_per_block, tokens_per_block
    assert b % sb == 0 and tp % tbs == 0 and tbs % chunk == 0
    assert sb == 1 or tbs == tp
    w_main = w_in[:, :5 * da].astype(BF16)
    w_gate = w_in[:, 5 * da:5 * da + 2 * nh]
    wgt = w_gate.T.astype(BF16)
    wgc = jnp.pad(w_gate, ((0, 0), (0, LANES - 2 * nh))).astype(BF16)
    bias = jnp.concatenate([b_i, b_f]).astype(F32)
    bgt = bias.reshape(2 * nh, 1)
    bgc = jnp.pad(bias, (0, LANES - 2 * nh)).reshape(1, LANES)
    full = lambda shape: pl.BlockSpec(shape, lambda i, j: (0,) * len(shape))
    kern = functools.partial(_mlstm_layer_kernel, n_heads=nh, chunk=chunk,
                             t_valid=None if t_valid == tp else t_valid)
    y, c1, n1, m1 = pl.pallas_call(
        kern,
        grid=(b // sb, tp // tbs),
        in_specs=[
            pl.BlockSpec((sb, tbs, d), lambda i, j: (i, j, 0)),
            full((1, d)), full((d, 5 * da)), full((2 * nh, d)), full((d, LANES)),
            full((2 * nh, 1)), full((1, LANES)), full((1, da)), full((da, d)),
            pl.BlockSpec((sb, nh, dh, dh), lambda i, j: (i, 0, 0, 0)),
            pl.BlockSpec((sb, nh, 1, dh), lambda i, j: (i, 0, 0, 0)),
            pl.BlockSpec((sb, nh, 1, 1), lambda i, j: (i, 0, 0, 0)),
        ],
        out_specs=[
            pl.BlockSpec((sb, tbs, d), lambda i, j: (i, j, 0)),
            pl.BlockSpec((sb, nh, dh, dh), lambda i, j: (i, 0, 0, 0)),
            pl.BlockSpec((sb, nh, 1, dh), lambda i, j: (i, 0, 0, 0)),
            pl.BlockSpec((sb, nh, 1, 1), lambda i, j: (i, 0, 0, 0)),
        ],
        out_shape=[
            jax.ShapeDtypeStruct((b, tp, d), F32),
            jax.ShapeDtypeStruct((b, nh, dh, dh), F32),
            jax.ShapeDtypeStruct((b, nh, 1, dh), F32),
            jax.ShapeDtypeStruct((b, nh, 1, 1), F32),
        ],
        scratch_shapes=[
            pltpu.VMEM((sb, nh, dh, 2 * dh), F32),
            pltpu.VMEM((sb, nh, 8, LANES), F32),
            pltpu.VMEM((sb * tbs, da), BF16),
        ],
        compiler_params=pltpu.CompilerParams(dimension_semantics=("arbitrary", "arbitrary"),
                                             vmem_limit_bytes=VMEM_LIMIT_BYTES),
        name="mlstm_layer",
    )(x, g.reshape(1, d), w_main, wgt, wgc, bgt, bgc, hnorm.reshape(1, da), w_out.astype(BF16),
      c0.astype(F32), n0.astype(F32).reshape(b, nh, 1, dh), m0.astype(F32).reshape(b, nh, 1, 1))
    return y, c1, n1.reshape(b, nh, dh), m1.reshape(b, nh)


def _rope_tables(pos, dk):
    half = dk // 2
    inv = ROPE_THETA ** (-(jnp.arange(half, dtype=F32) / half))
    ang = pos[:, None] * inv[None, :]
    cos, sin = jnp.cos(ang), jnp.sin(ang)
    reps = LANES // dk
    return (jnp.tile(jnp.concatenate([cos, cos], axis=-1), (1, reps)),
            jnp.tile(jnp.concatenate([-sin, sin], axis=-1), (1, reps)))


def _norm_proj_rope_kernel(x_ref, g_ref, w_ref, cos_ref, sin_ref, *out_refs, dk, rot_scale, mode):
    xb = _rmsnorm_rows(x_ref[...], g_ref[...]).astype(BF16)
    p = _dot(xb, w_ref[...])
    dr = p.shape[1] // 2
    a, rest = p[:, :dr], p[:, dr:]
    half = dk // 2
    reps = dr // LANES
    cos = jnp.tile(cos_ref[...], (1, reps))
    sin = jnp.tile(sin_ref[...], (1, reps))
    lane = lax.broadcasted_iota(jnp.int32, a.shape, 1)
    first = (lane & (dk - 1)) < half
    partner = jnp.where(first, pltpu.roll(a, dr - half, 1), pltpu.roll(a, half, 1))
    rot = a * cos + partner * sin
    if mode == "kv":
        k_ref, v_ref, kb_ref, vb_ref = out_refs
        k_ref[...] = rot
        v_ref[...] = rest
        kb_ref[...] = rot.astype(BF16)
        vb_ref[...] = rest.astype(BF16)
    else:
        qb_ref, z_ref = out_refs
        qb_ref[...] = (rot * rot_scale).astype(BF16)
        z_ref[...] = rest


def _norm_proj_rope(x, g, w, cos_t, sin_t, *, dk, mode, rot_scale=1.0, rows_per_block):
    n, d = x.shape
    nout = w.shape[1]
    dr = nout // 2
    tm = rows_per_block
    nblk_t = cos_t.shape[0] // tm
    assert n % tm == 0 and cos_t.shape[0] % tm == 0 and (dk & (dk - 1)) == 0
    row = lambda cols: pl.BlockSpec((tm, cols), lambda i: (i, 0))
    full = lambda shape: pl.BlockSpec(shape, lambda i: (0,) * len(shape))
    tab = pl.BlockSpec((tm, LANES), lambda i: (i % nblk_t, 0))
    if mode == "kv":
        out_specs = [row(dr), row(nout - dr), row(dr), row(nout - dr)]
        out_shape = [jax.ShapeDtypeStruct((n, dr), F32), jax.ShapeDtypeStruct((n, nout - dr), F32),
                     jax.ShapeDtypeStruct((n, dr), BF16), jax.ShapeDtypeStruct((n, nout - dr), BF16)]
    else:
        out_specs = [row(dr), row(nout - dr)]
        out_shape = [jax.ShapeDtypeStruct((n, dr), BF16), jax.ShapeDtypeStruct((n, nout - dr), F32)]
    return pl.pallas_call(
        functools.partial(_norm_proj_rope_kernel, dk=dk, rot_scale=rot_scale, mode=mode),
        grid=(n // tm,),
        in_specs=[row(d), full((1, d)), full((d, nout)), tab, tab],
        out_specs=out_specs,
        out_shape=out_shape,
        compiler_params=pltpu.CompilerParams(dimension_semantics=("arbitrary",),
                                             vmem_limit_bytes=VMEM_LIMIT_BYTES),
        name="norm_proj_rope_" + mode,
    )(x, g.reshape(1, d), w.astype(BF16), cos_t, sin_t)


def _lambda(lamp_ref, lam_init):
    lp = lamp_ref[...]
    s1 = jnp.sum(lp[0:1] * lp[1:2], axis=1, keepdims=True)
    s2 = jnp.sum(lp[2:3] * lp[3:4], axis=1, keepdims=True)
    return jnp.exp(s1) - jnp.exp(s2) + lam_init


def _split_subheads(q, dk):
    lane = lax.broadcasted_iota(jnp.int32, q.shape, 1)
    zero = jnp.zeros_like(q)
    return jnp.concatenate([jnp.where(lane < dk, q, zero), jnp.where(lane >= dk, q, zero)], axis=0)


def _softmax_step(s, v, m_ref, acc_ref):
    dv = v.shape[1]
    ones_col = (lax.broadcasted_iota(jnp.int32, v.shape, 1) == 0).astype(v.dtype)
    m_prev = m_ref[...]
    m_new = jnp.maximum(m_prev, jnp.max(s, axis=1, keepdims=True))
    alpha = jnp.exp(m_prev - m_new)
    p = jnp.exp(s - m_new).astype(BF16)
    acc_ref[...] = alpha * acc_ref[...] + _dot(p, jnp.concatenate([v, ones_col], axis=1))
    m_ref[...] = m_new


def _diff_finish(acc, tq, dv, lam, sub, z, lam_init):
    o_all = acc[:, 0:dv] / acc[:, dv:dv + 1]
    o = o_all[0:tq] - lam * o_all[tq:2 * tq]
    on = o * lax.rsqrt(jnp.mean(o * o, axis=-1, keepdims=True) + EPS) * sub * (1.0 - lam_init)
    return (on * (z * _sigmoid(z))).astype(BF16)


def _flash_diff_attn_kernel(q_ref, k_ref, v_ref, z_ref, sub_ref, lamp_ref, g_ref, q2_ref, m_ref, acc_ref,
                            *, dk, lam_init, row_chunk):
    qi = pl.program_id(2)
    tq = q_ref.shape[1]
    dv = v_ref.shape[2]
    tk = tq
    rc = row_chunk
    q2_ref[...] = _split_subheads(q_ref[0], dk)
    m_ref[...] = jnp.full(m_ref.shape, NEG_INF, F32)
    acc_ref[...] = jnp.zeros(acc_ref.shape, F32)

    def attend(k0, diagonal):
        for r in range(2 * tq // rc):
            rows = pl.ds(r * rc, rc)
            a = (r * rc) % tq
            kw = a + rc if diagonal else tk
            s = _dot_nt(q2_ref[rows, :], k_ref[0, pl.ds(k0, kw), :])
            if diagonal:
                qpos = a + lax.broadcasted_iota(jnp.int32, s.shape, 0)
                kpos = lax.broadcasted_iota(jnp.int32, s.shape, 1)
                s = jnp.where(qpos >= kpos, s, NEG_INF)
            _softmax_step(s, v_ref[0, pl.ds(k0, kw), :], m_ref.at[rows], acc_ref.at[rows])

    def full_block(kb, carry):
        attend(pl.multiple_of(kb * tk, tk), False)
        return carry

    lax.fori_loop(0, qi, full_block, 0)
    attend(pl.multiple_of(qi * tk, tk), True)

    lam = _lambda(lamp_ref, lam_init)
    g_ref[0] = _diff_finish(acc_ref[...], tq, dv, lam, sub_ref[...], z_ref[0], lam_init)


def _flash_diff_attn(qb, kb, vb, z, subln, lamp, *, dk, lam_init, q_block, row_chunk):
    b, t, qk = qb.shape
    dv = subln.shape[-1]
    nh = qk // (2 * dk)
    tq = q_block
    assert t % tq == 0 and 2 * dk == LANES and dv == LANES
    qspec = pl.BlockSpec((1, tq, dv), lambda i, h, j: (i, j, h))
    kvspec = pl.BlockSpec((1, t, dv), lambda i, h, j: (i, 0, h))
    full = lambda shape: pl.BlockSpec(shape, lambda i, h, j: (0,) * len(shape))
    return pl.pallas_call(
        functools.partial(_flash_diff_attn_kernel, dk=dk, lam_init=lam_init, row_chunk=row_chunk),
        grid=(b, nh, t // tq),
        in_specs=[qspec, kvspec, kvspec, qspec, full((1, dv)), full(lamp.shape)],
        out_specs=qspec,
        out_shape=jax.ShapeDtypeStruct((b, t, nh * dv), BF16),
        scratch_shapes=[pltpu.VMEM((2 * tq, dv), BF16), pltpu.VMEM((2 * tq, 1), F32),
                        pltpu.VMEM((2 * tq, 2 * dv), F32)],
        compiler_params=pltpu.CompilerParams(dimension_semantics=("arbitrary",) * 3,
                                             vmem_limit_bytes=VMEM_LIMIT_BYTES),
        name="flash_diff_attn",
    )(qb, kb, vb, z, subln.reshape(1, dv), lamp)


def _paged_diff_attn_kernel(pt_ref, q_ref, *refs, pages_per_step, n_heads, dk, t_valid, lam_init):
    del pt_ref
    pb = pages_per_step
    kp_refs, vp_refs = refs[:pb], refs[pb:2 * pb]
    kn_ref, vn_ref, z_ref, sub_ref, lamp_ref, g_ref, m_ref, acc_ref = refs[2 * pb:]
    step = pl.program_id(1)
    tq = q_ref.shape[1]
    dv = sub_ref.shape[1]

    @pl.when(step == 0)
    def _init():
        m_ref[...] = jnp.full(m_ref.shape, NEG_INF, F32)
        acc_ref[...] = jnp.zeros(acc_ref.shape, F32)

    q = q_ref[0]
    for h in range(n_heads):
        cols = slice(h * dv, (h + 1) * dv)
        q2 = _split_subheads(q[:, cols], dk)
        kt = jnp.concatenate([r[0, cols, :].astype(BF16) for r in kp_refs], axis=1)
        vh = jnp.concatenate([r[0, :, h, :].astype(BF16) for r in vp_refs], axis=0)
        _softmax_step(_dot(q2, kt), vh, m_ref.at[h], acc_ref.at[h])

    @pl.when(step == pl.num_programs(1) - 1)
    def _final():
        lam = _lambda(lamp_ref, lam_init)
        kn, vn, z = kn_ref[0], vn_ref[0], z_ref[0]
        for h in range(n_heads):
            cols = slice(h * dv, (h + 1) * dv)
            q2 = _split_subheads(q[:, cols], dk)
            s = _dot_nt(q2, kn[:, cols])
            qpos = lax.broadcasted_iota(jnp.int32, s.shape, 0)
            qpos = jnp.where(qpos >= tq, qpos - tq, qpos)
            kpos = lax.broadcasted_iota(jnp.int32, s.shape, 1)
            s = jnp.where((kpos <= qpos) & (kpos < t_valid), s, NEG_INF)
            _softmax_step(s, vn[:, cols], m_ref.at[h], acc_ref.at[h])
            g_ref[0, :, cols] = _diff_finish(acc_ref[h], tq, dv, lam, sub_ref[...], z[:, cols], lam_init)


def _paged_diff_attn(qb, cache_kt, cache_v, page_table, kn, vn, z, subln, lamp, *, dk, t_valid, lam_init,
                     pages_per_step):
    db, tq, width = qb.shape
    dv = subln.shape[-1]
    nh = width // dv
    page = cache_v.shape[1]
    n_pages = page_table.shape[1]
    pb = pages_per_step
    assert n_pages % pb == 0 and 2 * dk == dv

    def k_spec(j):
        return pl.BlockSpec((1, width, page), lambda i, s, pt: (pt[i, s * pb + j], 0, 0))

    def v_spec(j):
        return pl.BlockSpec((1, page, nh, dv), lambda i, s, pt: (pt[i, s * pb + j], 0, 0, 0))

    tok = pl.BlockSpec((1, tq, width), lambda i, s, pt: (i, 0, 0))
    full = lambda shape: pl.BlockSpec(shape, lambda i, s, pt: (0,) * len(shape))
    grid_spec = pltpu.PrefetchScalarGridSpec(
        num_scalar_prefetch=1,
        grid=(db, n_pages // pb),
        in_specs=([tok] + [k_spec(j) for j in range(pb)] + [v_spec(j) for j in range(pb)]
                  + [tok, tok, tok, full((1, dv)), full(lamp.shape)]),
        out_specs=tok,
        scratch_shapes=[pltpu.VMEM((nh, 2 * tq, 1), F32), pltpu.VMEM((nh, 2 * tq, 2 * dv), F32)],
    )
    return pl.pallas_call(
        functools.partial(_paged_diff_attn_kernel, pages_per_step=pb, n_heads=nh, dk=dk, t_valid=t_valid,
                          lam_init=lam_init),
        grid_spec=grid_spec,
        out_shape=jax.ShapeDtypeStruct((db, tq, width), BF16),
        compiler_params=pltpu.CompilerParams(dimension_semantics=("arbitrary", "arbitrary"),
                                             vmem_limit_bytes=VMEM_LIMIT_BYTES),
        name="paged_diff_attn",
    )(page_table, qb, *([cache_kt] * pb), *([cache_v] * pb), kn, vn, z, subln.reshape(1, dv), lamp)


def _out_proj_kernel(g_ref, w_ref, x_ref, gf_ref, y_ref, *, final):
    y = _dot(g_ref[...], w_ref[...]) + x_ref[...]
    if final:
        y = _rmsnorm_rows(y, gf_ref[...])
    y_ref[...] = y


def _out_proj(g, w, x, gf, *, final, rows_per_block):
    n, d = x.shape
    tm = rows_per_block
    assert n % tm == 0
    row = lambda cols: pl.BlockSpec((tm, cols), lambda i: (i, 0))
    full = lambda shape: pl.BlockSpec(shape, lambda i: (0,) * len(shape))
    return pl.pallas_call(
        functools.partial(_out_proj_kernel, final=final),
        grid=(n // tm,),
        in_specs=[row(g.shape[1]), full(w.shape), row(d), full((1, d))],
        out_specs=row(d),
        out_shape=jax.ShapeDtypeStruct((n, d), F32),
        compiler_params=pltpu.CompilerParams(dimension_semantics=("arbitrary",),
                                             vmem_limit_bytes=VMEM_LIMIT_BYTES),
        name="out_proj",
    )(g, w.astype(BF16), x, gf.reshape(1, d))


PROMPT_ROWS = 256
PROMPT_CHUNK = 128
PROMPT_Q_BLOCK = 1024
PROMPT_ROW_CHUNK = 128
SAMPLE_SEQ_PAD = 16
SAMPLE_SEQS_PER_BLOCK = 8
SAMPLE_Q_PAD = 8
PAGES_PER_STEP = 8


def _largest_divisor(n, cap):
    return max(k for k in range(1, cap + 1) if n % k == 0)


def kernel(x_prompt, x_sample, cache_k, cache_v, state_C, state_n, state_m, page_table, norm_g, w_in_a, b_i_a, b_f_a, hnorm_a, w_out_a, kv_norm_g, w_kv, w_in_b, lam_q1, lam_k1, lam_q2, lam_k2, subln_b, w_out_b, final_norm_g):
    n_a, n_b = w_in_a.shape[0], w_in_b.shape[0]
    depth = n_a + n_b
    nh_a = b_i_a.shape[-1]
    dh_a = w_out_a.shape[1] // nh_a
    dk = lam_q1.shape[-1]
    attn_scale = dk ** -0.5
    lamp = jnp.stack([lam_q1, lam_k1, lam_q2, lam_k2], axis=1).astype(F32)
    lam_inits = [0.8 - 0.6 * math.exp(-0.3 * (n_a + j)) for j in range(n_b)]

    def diff_layers(xf, cos_t, sin_t, rows, attend):
        for j in range(n_b):
            qb, z = _norm_proj_rope(xf, norm_g[n_a + j], w_in_b[j], cos_t, sin_t, dk=dk, mode="q",
                                    rot_scale=attn_scale, rows_per_block=rows)
            g = attend(qb, z, j)
            xf = _out_proj(g, w_out_b[j], xf, final_norm_g, final=(n_a + j == depth - 1), rows_per_block=rows)
        return xf

    bp, tp, d = x_prompt.shape
    rows_p = _largest_divisor(tp, PROMPT_ROWS)
    chunk_p = _largest_divisor(rows_p, PROMPT_CHUNK)
    x = x_prompt
    cs, ns, ms = [], [], []
    for l in range(n_a):
        x, c1, n1, m1 = _mlstm_layer(
            x, norm_g[l], w_in_a[l], b_i_a[l], b_f_a[l], hnorm_a[l], w_out_a[l],
            jnp.zeros((bp, nh_a, dh_a, dh_a), F32), jnp.zeros((bp, nh_a, dh_a), F32), jnp.zeros((bp, nh_a), F32),
            seqs_per_block=1, tokens_per_block=rows_p, chunk=chunk_p, t_valid=tp)
        cs.append(c1)
        ns.append(n1)
        ms.append(m1)
    cos_p, sin_p = _rope_tables(jnp.arange(tp, dtype=F32), dk)
    xf = x.reshape(bp * tp, d)
    k_p, v_p, kb, vb = _norm_proj_rope(xf, kv_norm_g, w_kv, cos_p, sin_p, dk=dk, mode="kv", rows_per_block=rows_p)
    kb3, vb3 = kb.reshape(bp, tp, -1), vb.reshape(bp, tp, -1)
    q_block = _largest_divisor(tp, PROMPT_Q_BLOCK)

    def attend_p(qb, z, j):
        g = _flash_diff_attn(qb.reshape(bp, tp, -1), kb3, vb3, z.reshape(bp, tp, -1), subln_b[j], lamp[j],
                             dk=dk, lam_init=lam_inits[j], q_block=q_block,
                             row_chunk=_largest_divisor(q_block, PROMPT_ROW_CHUNK))
        return g.reshape(bp * tp, -1)

    y_p = diff_layers(xf, cos_p, sin_p, rows_p, attend_p).reshape(bp, tp, d)
    xd = x_prompt.dtype
    c_p, n_p, m_p = jnp.stack(cs).astype(xd), jnp.stack(ns).astype(xd), jnp.stack(ms).astype(xd)

    db, ts, _ = x_sample.shape
    n_pages = page_table.shape[1]
    page = cache_k.shape[1]
    past_len = n_pages * page
    x = jnp.pad(x_sample, ((0, 0), (0, SAMPLE_SEQ_PAD - ts), (0, 0)))
    sb = _largest_divisor(db, SAMPLE_SEQS_PER_BLOCK)
    cs, ns, ms = [], [], []
    for l in range(n_a):
        x, c1, n1, m1 = _mlstm_layer(
            x, norm_g[l], w_in_a[l], b_i_a[l], b_f_a[l], hnorm_a[l], w_out_a[l],
            state_C[l], state_n[l], state_m[l],
            seqs_per_block=sb, tokens_per_block=SAMPLE_SEQ_PAD, chunk=SAMPLE_SEQ_PAD, t_valid=ts)
        cs.append(c1)
        ns.append(n1)
        ms.append(m1)
    xf = x[:, :ts].reshape(db * ts, d)
    pos_s = jnp.tile(past_len + jnp.arange(ts, dtype=F32), db)
    cos_s, sin_s = _rope_tables(pos_s, dk)
    rows_s = db * ts
    k_s, v_s, kbs, vbs = _norm_proj_rope(xf, kv_norm_g, w_kv, cos_s, sin_s, dk=dk, mode="kv", rows_per_block=rows_s)
    pad_q = lambda a: jnp.pad(a.reshape(db, ts, -1), ((0, 0), (0, SAMPLE_Q_PAD - ts), (0, 0)))
    kn, vn = pad_q(kbs), pad_q(vbs)
    ckt = jnp.transpose(cache_k, (0, 2, 3, 4, 1)).reshape(cache_k.shape[0], -1, page)
    pps = _largest_divisor(n_pages, PAGES_PER_STEP)

    def attend_s(qb, z, j):
        g = _paged_diff_attn(pad_q(qb), ckt, cache_v, page_table, kn, vn, pad_q(z), subln_b[j], lamp[j],
                             dk=dk, t_valid=ts, lam_init=lam_inits[j], pages_per_step=pps)
        return g[:, :ts].reshape(db * ts, -1)

    y_s = diff_layers(xf, cos_s, sin_s, rows_s, attend_s).reshape(db, ts, d)
    sd = state_C.dtype
    c_s, n_s, m_s = jnp.stack(cs).astype(sd), jnp.stack(ns).astype(sd), jnp.stack(ms).astype(sd)

    return (y_p, y_s,
            k_p.reshape(bp, tp, -1, 2, dk), v_p.reshape(bp, tp, -1, 2 * dk), c_p, n_p, m_p,
            k_s.reshape(db, ts, -1, 2, dk), v_s.reshape(db, ts, -1, 2 * dk), c_s, n_s, m_s)
```
